```python
import jax
import jax.numpy as jnp
from jax import lax
import numpy as np

D_MODEL = 1024
BATCH = 4
SEQ = 4096
DEPTH = 4
DEC_BATCH = 32
DEC_SEQ = 4
PAST_LEN = 8192
PAGE_SIZE = 128

HEAD_DIM = 64
N_HEADS = 12
N_MEM_HEADS = 4
N_MEM = 256
N_KV = 4
Q_BLOCK = 128
QW = N_HEADS * HEAD_DIM
KVW = N_KV * HEAD_DIM
MQW = N_MEM_HEADS * HEAD_DIM
MIX_WIDTH = QW + MQW
CMP_STRIDE = 16
CMP_LEN = 2 * CMP_STRIDE
SEL_BLOCK = 64
SEL_TOPK = 16
WINDOW = 512
N_IDX_HEADS = 8
IDX_DIM = 64
DSA_TOPK_MAX = 256
N_EXPERTS = 16
N_EXPERT_GROUPS = 4
EXPERTS_PER_GROUP = N_EXPERTS // N_EXPERT_GROUPS
MOE_TOPK = 2
D_EXPERT = 512
LN_EPS = 1e-5
DN_ALPHA = (2 * DEPTH) ** 0.25
DN_BETA = (8 * DEPTH) ** -0.25
N_MIXERS = 3
N_SB = (DEPTH + 2) // 3
N_NSA = (DEPTH + 1) // 3
N_DSA = DEPTH // 3
SB_SPLIT = (QW, QW, QW, MQW)
NSA_SPLIT = (QW, KVW, KVW, KVW, KVW, KVW, KVW, 3 * N_HEADS, MQW)
DSA_SPLIT = (QW, KVW, KVW, N_IDX_HEADS * IDX_DIM, IDX_DIM, N_IDX_HEADS, MQW)
F32 = jnp.float32

kernel_name = 'hybrid_sb_nsa_dsa_memxattn_moe_step'


def split_cols(h, widths):
    out, o = [], 0
    for w in widths:
        out.append(h[..., o:o + w])
        o += w
    return out


def heads(x, n, d=HEAD_DIM):
    return x.reshape(x.shape[:-1] + (n, d))


def layer_norm(x, g, b):
    xf = x.astype(F32)
    mu = xf.mean(-1, keepdims=True)
    var = jnp.square(xf - mu).mean(-1, keepdims=True)
    return ((xf - mu) * lax.rsqrt(var + LN_EPS) * g.astype(F32) + b.astype(F32)).astype(x.dtype)


def alibi_slopes(n):
    return jnp.exp2(-8.0 * jnp.arange(1, n + 1, dtype=F32) / n)


def masked_softmax(s, mask):
    s = jnp.where(mask, s, -jnp.inf)
    m = jnp.max(s, axis=-1, keepdims=True)
    m = jnp.where(jnp.isfinite(m), m, 0.0)
    e = jnp.where(mask, jnp.exp(s - m), 0.0)
    z = e.sum(-1, keepdims=True)
    return e / jnp.where(z > 0, z, 1.0)


def gather_pages(pool, page_table):
    g = pool[page_table]
    return g.reshape((g.shape[0], g.shape[1] * g.shape[2]) + g.shape[3:])


def sweep_query_blocks(fn, *arrs):
    b, t = arrs[0].shape[:2]
    nb = t // Q_BLOCK
    blocked = tuple(a.reshape((b, nb, Q_BLOCK) + a.shape[2:]).swapaxes(0, 1) for a in arrs)
    starts = jnp.arange(nb, dtype=jnp.int32) * Q_BLOCK
    out = lax.map(lambda xs: fn(xs[0], *xs[1:]), (starts,) + blocked)
    return out.swapaxes(0, 1).reshape((b, t) + out.shape[3:])


def stick_breaking(q, qpos, k, v, kpos):
    z = jnp.einsum('bqhd,bkhd->bhqk', q, k).astype(F32) * HEAD_DIM ** -0.5
    valid = kpos[None, :] < qpos[:, None]
    lsn = jnp.where(valid, jax.nn.log_sigmoid(-z), 0.0)
    after = lax.cumsum(lsn, axis=3, reverse=True) - lsn
    a = jnp.where(valid, jnp.exp(jax.nn.log_sigmoid(z) + after), 0.0)
    return jnp.einsum('bhqk,bkhd->bqhd', a.astype(v.dtype), v)


def sb_prompt(q, k, v):
    b, s = k.shape[:2]
    kpos = jnp.arange(s)

    def blk(start, qb):
        return stick_breaking(qb, start + jnp.arange(Q_BLOCK), k, v, kpos).reshape(b, Q_BLOCK, QW)
    return sweep_query_blocks(blk, q)


def sb_sample(q, k, v, pool_k, pool_v, page_table):
    k_all = jnp.concatenate([gather_pages(pool_k, page_table), k], axis=1)
    v_all = jnp.concatenate([gather_pages(pool_v, page_table), v], axis=1)
    qpos = PAST_LEN + jnp.arange(q.shape[1])
    out = stick_breaking(q, qpos, k_all, v_all, jnp.arange(k_all.shape[1]))
    return out.reshape(q.shape[0], q.shape[1], QW)


def nsa_compress(raw, pos, w1, w2):
    b, l, g, d = raw.shape
    c = raw.reshape(b, l // CMP_STRIDE, CMP_STRIDE, g, d)
    blocks = jnp.concatenate([c[:, :-1], c[:, 1:]], axis=2) + pos[None, None, :, None, :]
    flat = blocks.transpose(0, 1, 3, 2, 4).reshape(b, -1, g, CMP_LEN * d)
    return jax.nn.gelu(flat @ w1) @ w2


def to_sel_blocks(raw):
    b, l, g, d = raw.shape
    return raw.reshape(b, l // SEL_BLOCK, SEL_BLOCK, g, d).transpose(0, 3, 1, 2, 4)


def cmp_to_sel_overlap(n_cmp, n_sel):
    cs = jnp.arange(n_cmp) * CMP_STRIDE
    ss = jnp.arange(n_sel) * SEL_BLOCK
    ov = jnp.minimum(cs[:, None] + CMP_LEN, ss[None, :] + SEL_BLOCK) - jnp.maximum(cs[:, None], ss[None, :])
    return jnp.clip(ov, 0, None).astype(F32) / CMP_LEN


def nsa_attend(q, gates, qpos, kcmp, vcmp, ks_blk, vs_blk, kw, vw, wpos, slopes):
    b, tq = q.shape[:2]
    g = kcmp.shape[2]
    r = N_HEADS // g
    n_cmp, n_sel = kcmp.shape[1], ks_blk.shape[2]
    scale = HEAD_DIM ** -0.5
    qg = q.reshape(b, tq, g, r, HEAD_DIM)
    sl = slopes.reshape(1, 1, g, r, 1)
    qp = qpos.reshape(1, tq, 1, 1, 1)
    cmp_end = jnp.arange(n_cmp) * CMP_STRIDE + (CMP_LEN - 1)
    s_c = jnp.einsum('btgrd,bcgd->btgrc', qg, kcmp).astype(F32) * scale - sl * (qp - cmp_end).astype(F32)
    p_c = masked_softmax(s_c, cmp_end <= qp)
    o_c = jnp.einsum('btgrc,bcgd->btgrd', p_c.astype(vcmp.dtype), vcmp)
    imp = jnp.einsum('btgc,cs->btgs', p_c.sum(3), cmp_to_sel_overlap(n_cmp, n_sel))
    blk = jnp.arange(n_sel)
    qb = (qpos // SEL_BLOCK)[:, None]
    forced = (blk == 0) | (blk == qb) | (blk == qb - 1)
    allowed = blk * SEL_BLOCK <= qpos[:, None]
    imp = jnp.where(forced[None, :, None, :], jnp.inf, jnp.where(allowed[None, :, None, :], imp, -jnp.inf))
    n_pick = min(SEL_TOPK, n_sel)
    _, idx = lax.top_k(imp, n_pick)
    bi = jnp.arange(b)[:, None, None, None]
    gi = jnp.arange(g)[None, None, :, None]
    ksel = ks_blk[bi, gi, idx].reshape(b, tq, g, n_pick * SEL_BLOCK, HEAD_DIM)
    vsel = vs_blk[bi, gi, idx].reshape(b, tq, g, n_pick * SEL_BLOCK, HEAD_DIM)
    kpos = (idx[..., None] * SEL_BLOCK + jnp.arange(SEL_BLOCK)).reshape(b, tq, g, 1, n_pick * SEL_BLOCK)
    s_s = jnp.einsum('btgrd,btgkd->btgrk', qg, ksel).astype(F32) * scale - sl * (qp - kpos).astype(F32)
    p_s = masked_softmax(s_s, kpos <= qp)
    o_s = jnp.einsum('btgrk,btgkd->btgrd', p_s.astype(vsel.dtype), vsel)
    wp = wpos.reshape(1, 1, 1, 1, -1)
    s_w = jnp.einsum('btgrd,bwgd->btgrw', qg, kw).astype(F32) * scale - sl * (qp - wp).astype(F32)
    p_w = masked_softmax(s_w, (wp <= qp) & (qp - wp < WINDOW) & (wp >= 0))
    o_w = jnp.einsum('btgrw,bwgd->btgrd', p_w.astype(vw.dtype), vw)
    gt = gates.reshape(b, tq, g, r, 3)
    o = gt[..., 0:1] * o_c + gt[..., 1:2] * o_s + gt[..., 2:3] * o_w
    return o.reshape(b, tq, QW)


def nsa_prompt(q, gates, ck, cv, sk, sv, wk, wv, pos, w1, w2, slopes):
    kcmp = nsa_compress(ck, pos[0], w1[0], w2[0])
    vcmp = nsa_compress(cv, pos[1], w1[1], w2[1])
    ks_blk, vs_blk = to_sel_blocks(sk), to_sel_blocks(sv)
    pad = ((0, 0), (WINDOW, 0), (0, 0), (0, 0))
    kw_pad, vw_pad = jnp.pad(wk, pad), jnp.pad(wv, pad)
    span = WINDOW + Q_BLOCK

    def blk(start, qb, gb):
        kw = lax.dynamic_slice_in_dim(kw_pad, start, span, axis=1)
        vw = lax.dynamic_slice_in_dim(vw_pad, start, span, axis=1)
        wpos = start - WINDOW + jnp.arange(span)
        return nsa_attend(qb, gb, start + jnp.arange(Q_BLOCK), kcmp, vcmp, ks_blk, vs_blk, kw, vw, wpos, slopes)
    return sweep_query_blocks(blk, q, gates)


def nsa_sample(q, gates, ck, cv, sk, sv, wk, wv, pool_ck, pool_cv, pool_sk, pool_sv,
               win_k, win_v, page_table, pos, w1, w2, slopes):
    total = PAST_LEN + q.shape[1]
    lpad = -(-total // SEL_BLOCK) * SEL_BLOCK

    def full(pool, new):
        rows = jnp.concatenate([gather_pages(pool, page_table), new], axis=1)
        return jnp.pad(rows, ((0, 0), (0, lpad - total), (0, 0), (0, 0)))
    kcmp = nsa_compress(full(pool_ck, ck), pos[0], w1[0], w2[0])
    vcmp = nsa_compress(full(pool_cv, cv), pos[1], w1[1], w2[1])
    ks_blk, vs_blk = to_sel_blocks(full(pool_sk, sk)), to_sel_blocks(full(pool_sv, sv))
    wb = win_k.shape[1]
    kw = jnp.concatenate([win_k, wk], axis=1)
    vw = jnp.concatenate([win_v, wv], axis=1)
    wpos = PAST_LEN - wb + jnp.arange(kw.shape[1])
    qpos = PAST_LEN + jnp.arange(q.shape[1])
    out = nsa_attend(q, gates, qpos, kcmp, vcmp, ks_blk, vs_blk, kw, vw, wpos, slopes)
    return out, kw[:, kw.shape[1] - wb:], vw[:, vw.shape[1] - wb:]


def dsa_attend(q, iq, iw, qpos, k, v, ik, kpos, slopes, topk):
    b, tq = q.shape[:2]
    g = k.shape[2]
    r = N_HEADS // g
    logits = jnp.einsum('bthe,bse->bths', iq, ik).astype(F32) * IDX_DIM ** -0.5
    score = jnp.einsum('bths,bth->bts', jax.nn.relu(logits), iw.astype(F32)) * N_IDX_HEADS ** -0.5
    score = jnp.where(kpos[None, None, :] <= qpos[None, :, None], score, -jnp.inf)
    _, idx = lax.top_k(score, topk)
    bi = jnp.arange(b)[:, None, None]
    ksel, vsel = k[bi, idx], v[bi, idx]
    psel = kpos[idx].reshape(b, tq, 1, 1, topk)
    qg = q.reshape(b, tq, g, r, HEAD_DIM)
    sl = slopes.reshape(1, 1, g, r, 1)
    qp = qpos.reshape(1, tq, 1, 1, 1)
    s = jnp.einsum('btgrd,btkgd->btgrk', qg, ksel).astype(F32) * HEAD_DIM ** -0.5 - sl * (qp - psel).astype(F32)
    p = masked_softmax(s, psel <= qp)
    o = jnp.einsum('btgrk,btkgd->btgrd', p.astype(vsel.dtype), vsel)
    return o.reshape(b, tq, QW)


def dsa_prompt(q, iq, iw, k, v, ik, slopes):
    s = k.shape[1]
    kpos = jnp.arange(s)
    topk = min(DSA_TOPK_MAX, s // 4)

    def blk(start, qb, iqb, iwb):
        return dsa_attend(qb, iqb, iwb, start + jnp.arange(Q_BLOCK), k, v, ik, kpos, slopes, topk)
    return sweep_query_blocks(blk, q, iq, iw)


def dsa_sample(q, iq, iw, k, v, ik, pool_k, pool_v, pool_ik, page_table, slopes):
    k_all = jnp.concatenate([gather_pages(pool_k, page_table), k], axis=1)
    v_all = jnp.concatenate([gather_pages(pool_v, page_table), v], axis=1)
    ik_all = jnp.concatenate([gather_pages(pool_ik, page_table), ik], axis=1)
    total = k_all.shape[1]
    qpos = PAST_LEN + jnp.arange(q.shape[1])
    return dsa_attend(q, iq, iw, qpos, k_all, v_all, ik_all, jnp.arange(total), slopes,
                      min(DSA_TOPK_MAX, total // 4))


def mem_attend(mq, mk, mv):
    s = jnp.einsum('bthd,bmhd->bhtm', mq, mk).astype(F32) * HEAD_DIM ** -0.5
    p = jax.nn.softmax(s, axis=-1)
    o = jnp.einsum('bhtm,bmhd->bthd', p.astype(mv.dtype), mv)
    return o.reshape(o.shape[0], o.shape[1], MQW)


def moe_ffn(x, router_w, router_b, w_gate, w_up, w_down):
    shp = x.shape
    xf = x.reshape(-1, shp[-1])
    probs = jax.nn.softmax((xf @ router_w).astype(F32), axis=-1)
    biased = probs + router_b.astype(F32)
    grp = biased.reshape(-1, N_EXPERT_GROUPS, EXPERTS_PER_GROUP)
    grp_score = lax.top_k(grp, MOE_TOPK)[0].sum(-1)
    best = jnp.argmax(grp_score, axis=-1)
    in_grp = (jnp.arange(N_EXPERTS) // EXPERTS_PER_GROUP)[None, :] == best[:, None]
    _, eidx = lax.top_k(jnp.where(in_grp, biased, -jnp.inf), MOE_TOPK)
    wsel = jnp.take_along_axis(probs, eidx, axis=-1)
    wsel = wsel / wsel.sum(-1, keepdims=True)
    combine = jnp.einsum('nk,nke->ne', wsel, jax.nn.one_hot(eidx, N_EXPERTS, dtype=F32))
    h = jax.nn.silu(jnp.einsum('nd,edf->nef', xf, w_gate)) * jnp.einsum('nd,edf->nef', xf, w_up)
    y = jnp.einsum('nef,efd->nd', h * combine[:, :, None].astype(h.dtype), w_down)
    return y.reshape(shp)


def mix_residual(x, mix, memo, w_o, g, b):
    return layer_norm(DN_ALPHA * x + jnp.concatenate([mix, memo], axis=-1) @ w_o, g, b)


def setup_inputs(seed: int = 0) -> dict:
    key = jax.random.key(seed)
    counter = [0]

    def nrm(shape, scale):
        counter[0] += 1
        return jax.random.normal(jax.random.fold_in(key, counter[0]), shape, F32) * scale

    n_pages = PAST_LEN // PAGE_SIZE
    n_used = DEC_BATCH * n_pages
    n_pool = n_used + max(1, n_used // 4)
    wb = min(WINDOW, PAST_LEN)
    s = D_MODEL ** -0.5
    bs = DN_BETA * s

    def combined(n, widths, scales):
        return jnp.concatenate([nrm((n, D_MODEL, w), sc) for w, sc in zip(widths, scales)], axis=-1)

    inp = {}
    inp['x_prompt'] = nrm((BATCH, SEQ, D_MODEL), 1.0)
    inp['x_sample'] = nrm((DEC_BATCH, DEC_SEQ, D_MODEL), 1.0)
    inp['cache_sb_k'] = nrm((N_SB, n_pool, PAGE_SIZE, N_HEADS, HEAD_DIM), 1.0)
    inp['cache_sb_v'] = nrm((N_SB, n_pool, PAGE_SIZE, N_HEADS, HEAD_DIM), 1.0)
    for nm in ('ck', 'cv', 'sk', 'sv'):
        inp['cache_nsa_' + nm] = nrm((N_NSA, n_pool, PAGE_SIZE, N_KV, HEAD_DIM), 1.0)
    inp['state_nsa_wk'] = nrm((N_NSA, DEC_BATCH, wb, N_KV, HEAD_DIM), 1.0)
    inp['state_nsa_wv'] = nrm((N_NSA, DEC_BATCH, wb, N_KV, HEAD_DIM), 1.0)
    inp['cache_dsa_k'] = nrm((N_DSA, n_pool, PAGE_SIZE, N_KV, HEAD_DIM), 1.0)
    inp['cache_dsa_v'] = nrm((N_DSA, n_pool, PAGE_SIZE, N_KV, HEAD_DIM), 1.0)
    inp['cache_dsa_ik'] = nrm((N_DSA, n_pool, PAGE_SIZE, IDX_DIM), 1.0)
    inp['cache_mem_k'] = nrm((DEPTH, DEC_BATCH, N_MEM, N_MEM_HEADS, HEAD_DIM), 1.0)
    inp['cache_mem_v'] = nrm((DEPTH, DEC_BATCH, N_MEM, N_MEM_HEADS, HEAD_DIM), 1.0)
    counter[0] += 1
    perm = jax.random.permutation(jax.random.fold_in(key, counter[0]), n_pool)
    inp['page_table'] = perm[:n_used].reshape(DEC_BATCH, n_pages).astype(jnp.int32)
    inp['mem_prompt'] = nrm((BATCH, N_MEM, D_MODEL), 1.0)
    inp['sb_w_in'] = combined(N_SB, SB_SPLIT, (s, s, bs, s))
    inp['nsa_w_in'] = combined(N_NSA, NSA_SPLIT, (s, s, bs, s, bs, s, bs, s, s))
    inp['dsa_w_in'] = combined(N_DSA, DSA_SPLIT, (s, s, bs, s, s, s, s))
    inp['nsa_cmp_pos'] = nrm((N_NSA, 2, CMP_LEN, HEAD_DIM), 0.5)
    inp['nsa_cmp_w1'] = nrm((N_NSA, 2, CMP_LEN * HEAD_DIM, HEAD_DIM), (CMP_LEN * HEAD_DIM) ** -0.5)
    inp['nsa_cmp_w2'] = nrm((N_NSA, 2, HEAD_DIM, HEAD_DIM), HEAD_DIM ** -0.5)
    inp['w_mem_kv'] = combined(DEPTH, (MQW, MQW), (s, bs))
    inp['w_out'] = nrm((DEPTH, MIX_WIDTH, D_MODEL), DN_BETA * MIX_WIDTH ** -0.5)
    inp['ln_g'] = 1.0 + nrm((DEPTH, 2, D_MODEL), 0.02)
    inp['ln_b'] = nrm((DEPTH, 2, D_MODEL), 0.02)
    inp['router_w'] = nrm((D_MODEL, N_EXPERTS), s)
    inp['router_b'] = nrm((N_EXPERTS,), 0.01)
    inp['moe_w_gate'] = nrm((DEPTH, N_EXPERTS, D_MODEL, D_EXPERT), s)
    inp['moe_w_up'] = nrm((DEPTH, N_EXPERTS, D_MODEL, D_EXPERT), bs)
    inp['moe_w_down'] = nrm((DEPTH, N_EXPERTS, D_EXPERT, D_MODEL), DN_BETA * D_EXPERT ** -0.5)
    return inp


def reference(x_prompt, x_sample, cache_sb_k, cache_sb_v,
              cache_nsa_ck, cache_nsa_cv, cache_nsa_sk, cache_nsa_sv,
              state_nsa_wk, state_nsa_wv, cache_dsa_k, cache_dsa_v, cache_dsa_ik,
              cache_mem_k, cache_mem_v, page_table, mem_prompt,
              sb_w_in, nsa_w_in, dsa_w_in, nsa_cmp_pos, nsa_cmp_w1, nsa_cmp_w2,
              w_mem_kv, w_out, ln_g, ln_b, router_w, router_b,
              moe_w_gate, moe_w_up, moe_w_down):
    slopes = alibi_slopes(N_HEADS)
    xp, xs = x_prompt, x_sample
    st = {nm: [] for nm in (
        'sb_k_p', 'sb_v_p', 'sb_k_s', 'sb_v_s',
        'ck_p', 'cv_p', 'sk_p', 'sv_p', 'ck_s', 'cv_s', 'sk_s', 'sv_s',
        'wk_p', 'wv_p', 'wk_s', 'wv_s',
        'dk_p', 'dv_p', 'dik_p', 'dk_s', 'dv_s', 'dik_s', 'mk_p', 'mv_p')}
    for i in range(DEPTH):
        kind, j = i % N_MIXERS, i // N_MIXERS
        if kind == 0:
            q_p, k_p, v_p, mq_p = split_cols(xp @ sb_w_in[j], SB_SPLIT)
            q_s, k_s, v_s, mq_s = split_cols(xs @ sb_w_in[j], SB_SPLIT)
            q_p, k_p, v_p = heads(q_p, N_HEADS), heads(k_p, N_HEADS), heads(v_p, N_HEADS)
            q_s, k_s, v_s = heads(q_s, N_HEADS), heads(k_s, N_HEADS), heads(v_s, N_HEADS)
            mix_p = sb_prompt(q_p, k_p, v_p)
            mix_s = sb_sample(q_s, k_s, v_s, cache_sb_k[j], cache_sb_v[j], page_table)
            for nm, a in (('sb_k_p', k_p), ('sb_v_p', v_p), ('sb_k_s', k_s), ('sb_v_s', v_s)):
                st[nm].append(a)
        elif kind == 1:
            cols_p = split_cols(xp @ nsa_w_in[j], NSA_SPLIT)
            cols_s = split_cols(xs @ nsa_w_in[j], NSA_SPLIT)
            q_p = heads(cols_p[0], N_HEADS)
            q_s = heads(cols_s[0], N_HEADS)
            ck_p, cv_p, sk_p, sv_p, wk_p, wv_p = (heads(a, N_KV) for a in cols_p[1:7])
            ck_s, cv_s, sk_s, sv_s, wk_s, wv_s = (heads(a, N_KV) for a in cols_s[1:7])
            g_p, g_s = jax.nn.sigmoid(cols_p[7]), jax.nn.sigmoid(cols_s[7])
            mq_p, mq_s = cols_p[8], cols_s[8]
            mix_p = nsa_prompt(q_p, g_p, ck_p, cv_p, sk_p, sv_p, wk_p, wv_p,
                               nsa_cmp_pos[j], nsa_cmp_w1[j], nsa_cmp_w2[j], slopes)
            mix_s, nwk_s, nwv_s = nsa_sample(q_s, g_s, ck_s, cv_s, sk_s, sv_s, wk_s, wv_s,
                                             cache_nsa_ck[j], cache_nsa_cv[j], cache_nsa_sk[j], cache_nsa_sv[j],
                                             state_nsa_wk[j], state_nsa_wv[j], page_table,
                                             nsa_cmp_pos[j], nsa_cmp_w1[j], nsa_cmp_w2[j], slopes)
            wkeep = min(WINDOW, wk_p.shape[1])
            for nm, a in (('ck_p', ck_p), ('cv_p', cv_p), ('sk_p', sk_p), ('sv_p', sv_p),
                          ('ck_s', ck_s), ('cv_s', cv_s), ('sk_s', sk_s), ('sv_s', sv_s),
                          ('wk_p', wk_p[:, wk_p.shape[1] - wkeep:]), ('wv_p', wv_p[:, wv_p.shape[1] - wkeep:]),
                          ('wk_s', nwk_s), ('wv_s', nwv_s)):
                st[nm].append(a)
        else:
            q_p, k_p, v_p, iq_p, ik_p, iw_p, mq_p = split_cols(xp @ dsa_w_in[j], DSA_SPLIT)
            q_s, k_s, v_s, iq_s, ik_s, iw_s, mq_s = split_cols(xs @ dsa_w_in[j], DSA_SPLIT)
            q_p, k_p, v_p = heads(q_p, N_HEADS), heads(k_p, N_KV), heads(v_p, N_KV)
            q_s, k_s, v_s = heads(q_s, N_HEADS), heads(k_s, N_KV), heads(v_s, N_KV)
            iq_p, iq_s = heads(iq_p, N_IDX_HEADS, IDX_DIM), heads(iq_s, N_IDX_HEADS, IDX_DIM)
            mix_p = dsa_prompt(q_p, iq_p, iw_p, k_p, v_p, ik_p, slopes)
            mix_s = dsa_sample(q_s, iq_s, iw_s, k_s, v_s, ik_s,
                               cache_dsa_k[j], cache_dsa_v[j], cache_dsa_ik[j], page_table, slopes)
            for nm, a in (('dk_p', k_p), ('dv_p', v_p), ('dik_p', ik_p),
                          ('dk_s', k_s), ('dv_s', v_s), ('dik_s', ik_s)):
                st[nm].append(a)
        mk_p, mv_p = split_cols(mem_prompt @ w_mem_kv[i], (MQW, MQW))
        mk_p, mv_p = heads(mk_p, N_MEM_HEADS), heads(mv_p, N_MEM_HEADS)
        st['mk_p'].append(mk_p)
        st['mv_p'].append(mv_p)
        mem_p = mem_attend(heads(mq_p, N_MEM_HEADS), mk_p, mv_p)
        mem_s = mem_attend(heads(mq_s, N_MEM_HEADS), cache_mem_k[i], cache_mem_v[i])
        xp = mix_residual(xp, mix_p, mem_p, w_out[i], ln_g[i, 0], ln_b[i, 0])
        xs = mix_residual(xs, mix_s, mem_s, w_out[i], ln_g[i, 0], ln_b[i, 0])
        xp = layer_norm(DN_ALPHA * xp + moe_ffn(xp, router_w, router_b, moe_w_gate[i], moe_w_up[i], moe_w_down[i]),
                        ln_g[i, 1], ln_b[i, 1])
        xs = layer_norm(DN_ALPHA * xs + moe_ffn(xs, router_w, router_b, moe_w_gate[i], moe_w_up[i], moe_w_down[i]),
                        ln_g[i, 1], ln_b[i, 1])
    return (xp, xs,
            jnp.stack(st['sb_k_p']), jnp.stack(st['sb_v_p']), jnp.stack(st['sb_k_s']), jnp.stack(st['sb_v_s']),
            jnp.stack(st['ck_p']), jnp.stack(st['cv_p']), jnp.stack(st['sk_p']), jnp.stack(st['sv_p']),
            jnp.stack(st['ck_s']), jnp.stack(st['cv_s']), jnp.stack(st['sk_s']), jnp.stack(st['sv_s']),
            jnp.stack(st['wk_p']), jnp.stack(st['wv_p']), jnp.stack(st['wk_s']), jnp.stack(st['wv_s']),
            jnp.stack(st['dk_p']), jnp.stack(st['dv_p']), jnp.stack(st['dik_p']),
            jnp.stack(st['dk_s']), jnp.stack(st['dv_s']), jnp.stack(st['dik_s']),
            jnp.stack(st['mk_p']), jnp.stack(st['mv_p']))
```

```python
import functools

import numpy as np
import jax
import jax.numpy as jnp
from jax import lax
from jax.experimental import pallas as pl
from jax.experimental.pallas import tpu as pltpu

F32 = jnp.float32
BF16 = jnp.bfloat16
I32 = jnp.int32

D_MODEL = 1024
HEAD_DIM = 64
N_HEADS = 12
N_KV = 4
N_REP = N_HEADS // N_KV
N_MEM_HEADS = 4
QW = N_HEADS * HEAD_DIM
KVW = N_KV * HEAD_DIM
MQW = N_MEM_HEADS * HEAD_DIM
PAGE = 128
CMP_STRIDE = 16
CMP_LEN = 32
SEL_BLOCK = 64
SEL_TOPK = 16
WINDOW = 512
N_IDX_HEADS = 8
IDX_DIM = 64
DSA_TOPK_MAX = 256
N_EXPERTS = 16
GROUP_SIZE = 4
D_EXPERT = 512
LN_EPS = 1e-5
DEPTH = 4
DN_ALPHA = (2 * DEPTH) ** 0.25
LANES = 128
CHUNK = 256
NEG = -1e30
SB_SKIP = 110.0
INT_MIN = -(2 ** 31)
VMEM_LIMIT = 56 * 1024 * 1024

SLOPES = np.exp2(np.float32(-8.0) * np.arange(1, N_HEADS + 1, dtype=np.float32) / np.float32(N_HEADS))


def _cparams(*sem):
    return pltpu.CompilerParams(dimension_semantics=sem, vmem_limit_bytes=VMEM_LIMIT)


def _iota(shape, dim, dtype=I32):
    return lax.broadcasted_iota(dtype, shape, dim)


def _dot_nt(a, b):
    return lax.dot_general(a, b, (((1,), (1,)), ((), ())), preferred_element_type=F32)


def _dot(a, b):
    return jnp.dot(a, b, preferred_element_type=F32)


def _layer_norm(y, g, b):
    mu = jnp.mean(y, axis=-1, keepdims=True)
    d = y - mu
    var = jnp.mean(d * d, axis=-1, keepdims=True)
    return d * lax.rsqrt(var + LN_EPS) * g + b


def _proj_body(x_ref, *refs):
    n = len(refs) // 2
    xb = x_ref[...].astype(BF16)
    for w_ref, o_ref in zip(refs[:n], refs[n:]):
        o_ref[...] = _dot(xb, w_ref[...])


def _proj(x, ws, tm):
    rows, kdim = x.shape
    in_specs = [pl.BlockSpec((tm, kdim), lambda i: (i, 0))]
    in_specs += [pl.BlockSpec(w.shape, lambda i: (0, 0)) for w in ws]
    out_specs = [pl.BlockSpec((tm, w.shape[1]), lambda i: (i, 0)) for w in ws]
    out_shape = [jax.ShapeDtypeStruct((rows, w.shape[1]), F32) for w in ws]
    return pl.pallas_call(
        _proj_body, grid=(rows // tm,), in_specs=in_specs, out_specs=out_specs, out_shape=out_shape,
        compiler_params=_cparams("parallel"), name="proj")(x, *ws)


def _outln_body(x_ref, mix_ref, mem_ref, wa_ref, wb_ref, g_ref, b_ref, o_ref):
    y = _dot(mix_ref[...].astype(BF16), wa_ref[...]) + _dot(mem_ref[...].astype(BF16), wb_ref[...])
    o_ref[...] = _layer_norm(DN_ALPHA * x_ref[...] + y, g_ref[...], b_ref[...])


def _out_ln(x, mix, memo, wa, wb, g, b, tm):
    rows = x.shape[0]
    row = lambda w: pl.BlockSpec((tm, w), lambda i: (i, 0))
    full = lambda a: pl.BlockSpec(a.shape, lambda i: (0, 0))
    return pl.pallas_call(
        _outln_body, grid=(rows // tm,),
        in_specs=[row(D_MODEL), row(QW), row(MQW), full(wa), full(wb), full(g), full(b)],
        out_specs=row(D_MODEL), out_shape=jax.ShapeDtypeStruct((rows, D_MODEL), F32),
        compiler_params=_cparams("parallel"), name="out_ln")(x, mix, memo, wa, wb, g, b)


def _first_max(vals):
    m = functools.reduce(jnp.maximum, vals)
    out, taken = [], None
    for v in vals:
        is_m = v == m
        if taken is None:
            out.append(is_m)
            taken = is_m
        else:
            out.append(is_m & jnp.logical_not(taken))
            taken = taken | is_m
    return out


def _route(logits, rb):
    m = jnp.max(logits, axis=0, keepdims=True)
    ex = jnp.exp(logits - m)
    probs = ex / jnp.sum(ex, axis=0, keepdims=True)
    biased = probs + rb
    p = [probs[e:e + 1, :] for e in range(N_EXPERTS)]
    bsd = [biased[e:e + 1, :] for e in range(N_EXPERTS)]
    gscore = []
    for gi in range(N_EXPERTS // GROUP_SIZE):
        a, b, c, d = bsd[4 * gi:4 * gi + 4]
        hi1, lo1 = jnp.maximum(a, b), jnp.minimum(a, b)
        hi2, lo2 = jnp.maximum(c, d), jnp.minimum(c, d)
        top1 = jnp.maximum(hi1, hi2)
        top2 = jnp.maximum(jnp.minimum(hi1, hi2), jnp.maximum(lo1, lo2))
        gscore.append(top1 + top2)
    gsel = _first_max(gscore)
    cand = [jnp.where(gsel[e // GROUP_SIZE], bsd[e], -jnp.inf) for e in range(N_EXPERTS)]
    oh1 = _first_max(cand)
    cand2 = [jnp.where(oh1[e], -jnp.inf, cand[e]) for e in range(N_EXPERTS)]
    oh2 = _first_max(cand2)
    w1 = functools.reduce(jnp.add, [jnp.where(oh1[e], p[e], 0.0) for e in range(N_EXPERTS)])
    w2 = functools.reduce(jnp.add, [jnp.where(oh2[e], p[e], 0.0) for e in range(N_EXPERTS)])
    tot = w1 + w2
    w1n, w2n = w1 / tot, w2 / tot
    return [jnp.where(oh1[e], w1n, 0.0) + jnp.where(oh2[e], w2n, 0.0) for e in range(N_EXPERTS)]


def _moe_body(x_ref, rwt_ref, rb_ref, wg_ref, wu_ref, wd_ref, g_ref, b_ref, o_ref, xb_ref, comb_ref, acc_ref):
    e = pl.program_id(1)
    tm = x_ref.shape[0]

    @pl.when(e == 0)
    def _():
        xb = x_ref[...].astype(BF16)
        xb_ref[...] = xb
        comb = _route(_dot_nt(rwt_ref[...], xb), rb_ref[...])
        rowi = _iota((LANES, 1), 0)
        ct = jnp.zeros((LANES, tm), F32)
        for k in range(N_EXPERTS):
            ct = ct + jnp.where(rowi == k, comb[k], 0.0)
        comb_ref[...] = ct.T
        acc_ref[...] = jnp.zeros_like(acc_ref)

    xb = xb_ref[...]
    hg = _dot(xb, wg_ref[...].astype(BF16))
    hu = _dot(xb, wu_ref[...].astype(BF16))
    h = hg * (1.0 / (1.0 + jnp.exp(-hg))) * hu
    lane = _iota((1, LANES), 1)
    col = jnp.sum(jnp.where(lane == e, comb_ref[...], 0.0), axis=1, keepdims=True)
    acc_ref[...] += _dot((h * col).astype(BF16), wd_ref[...].astype(BF16))

    @pl.when(e == N_EXPERTS - 1)
    def _():
        o_ref[...] = _layer_norm(DN_ALPHA * x_ref[...] + acc_ref[...], g_ref[...], b_ref[...])


def _moe_ln(x, rwt, rb, wg, wu, wd, g, b, tm):
    rows = x.shape[0]
    full = lambda a: pl.BlockSpec(a.shape, lambda i, e: (0, 0))
    return pl.pallas_call(
        _moe_body, grid=(rows // tm, N_EXPERTS),
        in_specs=[pl.BlockSpec((tm, D_MODEL), lambda i, e: (i, 0)), full(rwt), full(rb),
                  pl.BlockSpec((None, D_MODEL, D_EXPERT), lambda i, e: (e, 0, 0)),
                  pl.BlockSpec((None, D_MODEL, D_EXPERT), lambda i, e: (e, 0, 0)),
                  pl.BlockSpec((None, D_EXPERT, D_MODEL), lambda i, e: (e, 0, 0)),
                  full(g), full(b)],
        out_specs=pl.BlockSpec((tm, D_MODEL), lambda i, e: (i, 0)),
        out_shape=jax.ShapeDtypeStruct((rows, D_MODEL), F32),
        scratch_shapes=[pltpu.VMEM((tm, D_MODEL), BF16), pltpu.VMEM((tm, LANES), F32),
                        pltpu.VMEM((tm, D_MODEL), F32)],
        compiler_params=_cparams("parallel", "arbitrary"), name="moe_ln")(x, rwt, rb, wg, wu, wd, g, b)


def _mem_body(q_ref, k_ref, v_ref, o_ref):
    lane = _iota((1, MQW), 1)
    kb = k_ref[...].astype(BF16)
    v = v_ref[...]
    q = q_ref[...] * (HEAD_DIM ** -0.5)
    acc = jnp.zeros(q.shape, F32)
    for h in range(N_MEM_HEADS):
        hm = (lane >= h * HEAD_DIM) & (lane < (h + 1) * HEAD_DIM)
        s = _dot_nt(jnp.where(hm, q, 0.0).astype(BF16), kb)
        ex = jnp.exp(s - jnp.max(s, axis=1, keepdims=True))
        p = ex / jnp.sum(ex, axis=1, keepdims=True)
        acc = acc + _dot(p.astype(BF16), jnp.where(hm, v, 0.0).astype(BF16))
    o_ref[...] = acc


def _mem_attn(mq, mk, mv, tq):
    bsz, tlen, _ = mq.shape
    nmem = mk.shape[1]
    return pl.pallas_call(
        _mem_body, grid=(bsz, tlen // tq),
        in_specs=[pl.BlockSpec((None, tq, MQW), lambda b, i: (b, i, 0)),
                  pl.BlockSpec((None, nmem, MQW), lambda b, i: (b, 0, 0)),
                  pl.BlockSpec((None, nmem, MQW), lambda b, i: (b, 0, 0))],
        out_specs=pl.BlockSpec((None, tq, MQW), lambda b, i: (b, i, 0)),
        out_shape=jax.ShapeDtypeStruct((bsz, tlen, MQW), F32),
        compiler_params=_cparams("parallel", "parallel"), name="mem_attn")(mq, mk, mv)


def _sb_body(q_ref, k_ref, v_ref, o_ref, qj_ref, carry_ref, acc_ref, *, tq, tk, q_off, nk):
    i = pl.program_id(2)
    q0 = q_off + i * tq
    qpos = (q0 + _iota((tq, 1), 0)).astype(F32)
    lane = _iota((1, CHUNK), 1)
    kt_hi = jnp.clip((q0 + tq - 2) // tk, 0, nk - 1)
    later = jnp.where(_iota((tk, tk), 0) > _iota((tk, tk), 1), 1.0, 0.0).astype(BF16)
    o_ref[...] = jnp.zeros_like(o_ref)

    def head(j, c):
        hm = (lane >= j * HEAD_DIM) & (lane < (j + 1) * HEAD_DIM)
        qj_ref[...] = jnp.where(hm, q_ref[...] * (HEAD_DIM ** -0.5), 0.0).astype(BF16)
        carry_ref[...] = jnp.zeros_like(carry_ref)
        acc_ref[...] = jnp.zeros_like(acc_ref)

        def cond(st):
            return (st[0] >= 0) & (st[1] > -SB_SKIP)

        def body(st):
            kt = st[0]
            ks = pl.multiple_of(kt * tk, tk)
            kb = k_ref[pl.ds(ks, tk), :].astype(BF16)
            vb = jnp.where(hm, v_ref[pl.ds(ks, tk), :], 0.0).astype(BF16)
            z = _dot_nt(qj_ref[...], kb)
            kpos = (ks + _iota((1, tk), 1)).astype(F32)
            valid = kpos < qpos
            lp = jnp.log1p(jnp.exp(-jnp.abs(z)))
            lsn = jnp.where(valid, -(jnp.maximum(z, 0.0) + lp), 0.0)
            ls_hi = lsn.astype(BF16)
            ls_lo = (lsn - ls_hi.astype(F32)).astype(BF16)
            after = _dot(ls_hi, later) + _dot(ls_lo, later) + carry_ref[...]
            a = jnp.where(valid, jnp.exp(jnp.minimum(z, 0.0) - lp + after), 0.0)
            acc_ref[...] += _dot(a.astype(BF16), vb)
            cnew = carry_ref[...] + jnp.sum(lsn, axis=1, keepdims=True)
            carry_ref[...] = cnew
            return kt - 1, jnp.max(cnew)

        lax.while_loop(cond, body, (kt_hi, jnp.float32(0.0)))
        o_ref[...] += acc_ref[...]
        return c

    lax.fori_loop(0, CHUNK // HEAD_DIM, head, 0)


def _sb_attn(q, k, v, q_off, tq, tk):
    bsz, tlen, _ = q.shape
    klen = k.shape[1]
    nk = klen // tk
    body = functools.partial(_sb_body, tq=tq, tk=tk, q_off=q_off, nk=nk)
    return pl.pallas_call(
        body, grid=(bsz, QW // CHUNK, tlen // tq),
        in_specs=[pl.BlockSpec((None, tq, CHUNK), lambda b, c, i: (b, i, c)),
                  pl.BlockSpec((None, klen, CHUNK), lambda b, c, i: (b, 0, c)),
                  pl.BlockSpec((None, klen, CHUNK), lambda b, c, i: (b, 0, c))],
        out_specs=pl.BlockSpec((None, tq, CHUNK), lambda b, c, i: (b, i, c)),
        out_shape=jax.ShapeDtypeStruct((bsz, tlen, QW), F32),
        scratch_shapes=[pltpu.VMEM((tq, CHUNK), BF16), pltpu.VMEM((tq, 1), F32), pltpu.VMEM((tq, CHUNK), F32)],
        compiler_params=_cparams("parallel", "parallel", "arbitrary"), name="sb_attn")(q, k, v)


def _sortable(x):
    b = lax.bitcast_convert_type(x + 0.0, I32)
    return b ^ ((b >> 31) & 0x7FFFFFFF)


def _count(cond):
    c = jnp.sum(jnp.where(cond, 1.0, 0.0), axis=-1, keepdims=True)
    if c.ndim == 3:
        c = jnp.sum(c, axis=0, keepdims=True)
    return c


def _topk_mask(key, idx, kk, idx_bits):
    cshape = key.shape[-2:-1] + (1,) if key.ndim == 2 else (1, key.shape[1], 1)

    def vbody(it, tu):
        cand = tu | lax.shift_left(jnp.int32(1), 31 - it)
        cnt = _count(key >= (cand ^ INT_MIN))
        return jnp.where(cnt >= kk, cand, tu)

    tu = lax.fori_loop(0, 32, vbody, jnp.zeros(cshape, I32))
    ts = tu ^ INT_MIN
    gt = key > ts
    eq = key == ts
    need = kk - _count(gt)

    def jbody(it, j):
        cand = j | lax.shift_left(jnp.int32(1), idx_bits - 1 - it)
        cnt = _count(eq & (idx < cand))
        return jnp.where(cnt <= need, cand, j)

    j = lax.fori_loop(0, idx_bits, jbody, jnp.zeros(cshape, I32))
    return gt | (eq & (idx < j))


def _group_queries(q_ref, qg_ref, gm, tq):
    for r in range(N_REP):
        qr = q_ref[:, r * CHUNK:(r + 1) * CHUNK] * (HEAD_DIM ** -0.5)
        qg_ref[r * tq:(r + 1) * tq, :] = jnp.where(gm, qr, 0.0).astype(BF16)


def _alibi_body(slopes_ref, q_ref, k_ref, v_ref, *rest, mode, tq, tk, q_off, kbase, n_valid, nk, nsp):
    if mode in ("sel", "mask"):
        x_ref, o_ref, qg_ref, m_ref, l_ref, acc_ref = rest
    else:
        o_ref, qg_ref, m_ref, l_ref, acc_ref = rest
    i = pl.program_id(1)
    g = pl.program_id(2)
    lane = _iota((1, CHUNK), 1)
    gm = (lane >= g * HEAD_DIM) & (lane < (g + 1) * HEAD_DIM)
    _group_queries(q_ref, qg_ref, gm, tq)
    m_ref[...] = jnp.full(m_ref.shape, NEG, F32)
    l_ref[...] = jnp.zeros_like(l_ref)
    acc_ref[...] = jnp.zeros_like(acc_ref)
    q0 = q_off + i * tq
    qpos = (q0 + _iota((tq, 1), 0)).astype(F32)
    hi = jnp.clip((q0 + tq - 1 - kbase) // tk + 1, 0, nk)
    if mode == "win":
        lo = jnp.clip(jnp.maximum(q0 - (WINDOW - 1) - kbase, 0) // tk, 0, nk)
    else:
        lo = 0

    def body(kt, c):
        ks = pl.multiple_of(kt * tk, tk)
        kb = k_ref[pl.ds(ks, tk), :].astype(BF16)
        vb = v_ref[pl.ds(ks, tk), :].astype(BF16)
        s = _dot_nt(qg_ref[...], kb)
        kidx = ks + _iota((1, tk), 1)
        d = qpos - (kidx + kbase).astype(F32)
        msk = d >= 0.0
        if mode == "win":
            msk = msk & (d < float(WINDOW))
        if n_valid < nk * tk:
            msk = msk & (kidx < n_valid)
        if mode == "sel":
            expand = jnp.where(_iota((nsp, 1), 0) == (kidx >> 6), 1.0, 0.0).astype(BF16)
            msk = msk & (_dot(x_ref[...].astype(BF16), expand) > 0.5)
        if mode == "mask":
            msk = msk & (x_ref[kt] > 0.5)
        for r in range(N_REP):
            rows = slice(r * tq, (r + 1) * tq)
            sr = jnp.where(msk, s[rows] - slopes_ref[N_REP * g + r] * d, NEG)
            m_old = m_ref[rows]
            m_new = jnp.maximum(m_old, jnp.max(sr, axis=1, keepdims=True))
            p = jnp.where(msk, jnp.exp(sr - m_new), 0.0)
            alpha = jnp.exp(m_old - m_new)
            l_ref[rows] = alpha * l_ref[rows] + jnp.sum(p, axis=1, keepdims=True)
            acc_ref[rows] = alpha * acc_ref[rows] + _dot(p.astype(BF16), vb)
            m_ref[rows] = m_new
        return c

    lax.fori_loop(lo, hi, body, 0)
    for r in range(N_REP):
        rows = slice(r * tq, (r + 1) * tq)
        l = l_ref[rows]
        o = jnp.where(gm, acc_ref[rows] / jnp.where(l > 0.0, l, 1.0), 0.0)
        cols = slice(r * CHUNK, (r + 1) * CHUNK)

        @pl.when(g == 0)
        def _():
            o_ref[:, cols] = o

        @pl.when(g > 0)
        def _():
            o_ref[:, cols] += o


def _alibi_attn(q3, k, v, extra, *, mode, q_off, kbase, n_valid, tq, tk):
    bsz, tlen, _ = q3.shape
    klen = k.shape[1]
    nk = klen // tk
    nsp = extra.shape[-1] if mode == "sel" else 0
    body = functools.partial(_alibi_body, mode=mode, tq=tq, tk=tk, q_off=q_off, kbase=kbase,
                             n_valid=n_valid, nk=nk, nsp=nsp)
    in_specs = [pl.BlockSpec(memory_space=pltpu.SMEM),
                pl.BlockSpec((None, tq, QW), lambda b, i, g: (b, i, 0)),
                pl.BlockSpec((None, klen, KVW), lambda b, i, g: (b, 0, 0)),
                pl.BlockSpec((None, klen, KVW), lambda b, i, g: (b, 0, 0))]
    args = [jnp.asarray(SLOPES), q3, k, v]
    if mode == "sel":
        in_specs.append(pl.BlockSpec((None, None, tq, nsp), lambda b, i, g: (b, g, i, 0)))
        args.append(extra)
    if mode == "mask":
        in_specs.append(pl.BlockSpec((None, nk, tq, tk), lambda b, i, g: (b, 0, i, 0)))
        args.append(extra)
    return pl.pallas_call(
        body, grid=(bsz, tlen // tq, N_KV), in_specs=in_specs,
        out_specs=pl.BlockSpec((None, tq, QW), lambda b, i, g: (b, i, 0)),
        out_shape=jax.ShapeDtypeStruct((bsz, tlen, QW), F32),
        scratch_shapes=[pltpu.VMEM((N_REP * tq, CHUNK), BF16), pltpu.VMEM((N_REP * tq, 1), F32),
                        pltpu.VMEM((N_REP * tq, 1), F32), pltpu.VMEM((N_REP * tq, CHUNK), F32)],
        compiler_params=_cparams("parallel", "parallel", "arbitrary"), name="alibi_" + mode)(*args)


def _nsa_cmp_body(slopes_ref, q_ref, k_ref, v_ref, ov_ref, o_ref, sel_ref, qg_ref, *, tq, q_off, n_cmp, n_sel, idx_bits):
    i = pl.program_id(1)
    g = pl.program_id(2)
    ncp = k_ref.shape[0]
    nsp = ov_ref.shape[1]
    lane = _iota((1, CHUNK), 1)
    gm = (lane >= g * HEAD_DIM) & (lane < (g + 1) * HEAD_DIM)
    _group_queries(q_ref, qg_ref, gm, tq)
    qpos_i = q_off + i * tq + _iota((tq, 1), 0)
    qpos = qpos_i.astype(F32)
    cidx = _iota((1, ncp), 1)
    d = qpos - (cidx * CMP_STRIDE + (CMP_LEN - 1)).astype(F32)
    msk = (d >= 0.0) & (cidx < n_cmp)
    s = _dot_nt(qg_ref[...], k_ref[...].astype(BF16))
    vb = v_ref[...].astype(BF16)
    psum = jnp.zeros((tq, ncp), F32)
    for r in range(N_REP):
        rows = slice(r * tq, (r + 1) * tq)
        sr = jnp.where(msk, s[rows] - slopes_ref[N_REP * g + r] * d, NEG)
        ex = jnp.where(msk, jnp.exp(sr - jnp.max(sr, axis=1, keepdims=True)), 0.0)
        z = jnp.sum(ex, axis=1, keepdims=True)
        p = ex / jnp.where(z > 0.0, z, 1.0)
        psum = psum + p
        o = jnp.where(gm, _dot(p.astype(BF16), vb), 0.0)
        cols = slice(r * CHUNK, (r + 1) * CHUNK)

        @pl.when(g == 0)
        def _():
            o_ref[:, cols] = o

        @pl.when(g > 0)
        def _():
            o_ref[:, cols] += o

    ps_hi = psum.astype(BF16)
    ps_lo = (psum - ps_hi.astype(F32)).astype(BF16)
    imp = _dot(ps_hi, ov_ref[...]) + _dot(ps_lo, ov_ref[...])
    blk = _iota((1, nsp), 1)
    qb = qpos_i >> 6
    forced = (blk == 0) | (blk == qb) | (blk == qb - 1)
    allowed = blk * SEL_BLOCK <= qpos_i
    score = jnp.where(forced, jnp.inf, jnp.where(allowed, imp, -jnp.inf))
    key = jnp.where(blk < n_sel, _sortable(score), INT_MIN)
    picked = _topk_mask(key, jnp.broadcast_to(blk, key.shape), float(min(SEL_TOPK, n_sel)), idx_bits)
    sel_ref[...] = jnp.where(picked, 1.0, 0.0)


def _nsa_cmp(q3, kcmp, vcmp, ov, *, q_off, n_cmp, n_sel, tq):
    bsz, tlen, _ = q3.shape
    ncp = kcmp.shape[1]
    nsp = ov.shape[1]
    idx_bits = int(nsp).bit_length()
    body = functools.partial(_nsa_cmp_body, tq=tq, q_off=q_off, n_cmp=n_cmp, n_sel=n_sel, idx_bits=idx_bits)
    return pl.pallas_call(
        body, grid=(bsz, tlen // tq, N_KV),
        in_specs=[pl.BlockSpec(memory_space=pltpu.SMEM),
                  pl.BlockSpec((None, tq, QW), lambda b, i, g: (b, i, 0)),
                  pl.BlockSpec((None, ncp, KVW), lambda b, i, g: (b, 0, 0)),
                  pl.BlockSpec((None, ncp, KVW), lambda b, i, g: (b, 0, 0)),
                  pl.BlockSpec((ncp, nsp), lambda b, i, g: (0, 0))],
        out_specs=[pl.BlockSpec((None, tq, QW), lambda b, i, g: (b, i, 0)),
                   pl.BlockSpec((None, None, tq, nsp), lambda b, i, g: (b, g, i, 0))],
        out_shape=[jax.ShapeDtypeStruct((bsz, tlen, QW), F32),
                   jax.ShapeDtypeStruct((bsz, N_KV, tlen, nsp), F32)],
        scratch_shapes=[pltpu.VMEM((N_REP * tq, CHUNK), BF16)],
        compiler_params=_cparams("parallel", "parallel", "arbitrary"), name="nsa_cmp")(
            jnp.asarray(SLOPES), q3, kcmp, vcmp, ov)


def _compress_body(a_ref, nx_ref, pa_ref, pb_ref, w1a_ref, w1b_ref, w2_ref, o_ref, *, rows, n_cmp):
    i = pl.program_id(1)
    a = a_ref[...]
    rowi = _iota((rows, 1), 0)
    b = jnp.where(rowi == rows - 1, nx_ref[0:1, :], pltpu.roll(a, rows - 1, 0))
    h = _dot((a + pa_ref[...]).astype(BF16), w1a_ref[...]) + _dot((b + pb_ref[...]).astype(BF16), w1b_ref[...])
    gl = 0.5 * h * (1.0 + jnp.tanh(0.7978845608028654 * (h + 0.044715 * (h * h * h))))
    o = _dot(gl.astype(BF16), w2_ref[...])
    o_ref[...] = jnp.where(i * rows + rowi < n_cmp, o, 0.0)


def _compress(c2, pa, pb, w1a, w1b, w2e, n_cmp, rows):
    bsz, nrow, wid = c2.shape
    nblk = nrow // rows
    full = lambda a: pl.BlockSpec(a.shape, lambda b, i: (0, 0))
    body = functools.partial(_compress_body, rows=rows, n_cmp=n_cmp)
    return pl.pallas_call(
        body, grid=(bsz, nblk),
        in_specs=[pl.BlockSpec((None, rows, wid), lambda b, i: (b, i, 0)),
                  pl.BlockSpec((None, rows, wid), lambda b, i: (b, jnp.minimum(i + 1, nblk - 1), 0)),
                  full(pa), full(pb), full(w1a), full(w1b), full(w2e)],
        out_specs=pl.BlockSpec((None, rows, KVW), lambda b, i: (b, i, 0)),
        out_shape=jax.ShapeDtypeStruct((bsz, nrow, KVW), F32),
        compiler_params=_cparams("parallel", "parallel"), name="nsa_compress")(c2, c2, pa, pb, w1a, w1b, w2e)


def _gate_body(gl_ref, oc_ref, os_ref, ow_ref, o_ref):
    acc = None
    for j, ref in enumerate((oc_ref, os_ref, ow_ref)):
        gate = 1.0 / (1.0 + jnp.exp(-gl_ref[:, j * QW:(j + 1) * QW]))
        acc = gate * ref[...] if acc is None else acc + gate * ref[...]
    o_ref[...] = acc


def _gate_combine(gl, oc, osel, ow, tm):
    rows = gl.shape[0]
    row = lambda w: pl.BlockSpec((tm, w), lambda i: (i, 0))
    return pl.pallas_call(
        _gate_body, grid=(rows // tm,), in_specs=[row(3 * QW), row(QW), row(QW), row(QW)],
        out_specs=row(QW), out_shape=jax.ShapeDtypeStruct((rows, QW), F32),
        compiler_params=_cparams("parallel"), name="nsa_gate")(gl, oc, osel, ow)


def _dsa_index_body(iq_ref, ik_ref, iw_ref, o_ref, sc_ref, *, tq, tk, q_off, nk, topk, idx_bits):
    i = pl.program_id(1)
    q0 = q_off + i * tq
    qpos = (q0 + _iota((tq, 1), 0)).astype(F32)
    iqb = iq_ref[...].reshape(N_IDX_HEADS * tq, IDX_DIM).astype(BF16)
    iw = iw_ref[...]
    hi = jnp.clip((q0 + tq - 1) // tk + 1, 0, nk)
    sc_ref[...] = jnp.full(sc_ref.shape, -jnp.inf, F32)

    def body(kt, c):
        ks = pl.multiple_of(kt * tk, tk)
        lg = _dot_nt(iqb, ik_ref[pl.ds(ks, tk), :].astype(BF16)) * (IDX_DIM ** -0.5)
        sc = jnp.zeros((tq, tk), F32)
        for h in range(N_IDX_HEADS):
            sc = sc + jnp.maximum(lg[h * tq:(h + 1) * tq], 0.0) * iw[:, h:h + 1]
        sc = sc * (N_IDX_HEADS ** -0.5)
        kpos = (ks + _iota((1, tk), 1)).astype(F32)
        sc_ref[kt] = jnp.where(kpos <= qpos, sc, -jnp.inf)
        return c

    lax.fori_loop(0, hi, body, 0)
    key = _sortable(sc_ref[...])
    idx = _iota(key.shape, 0) * tk + _iota(key.shape, 2)
    o_ref[...] = jnp.where(_topk_mask(key, idx, float(topk), idx_bits), 1.0, 0.0)


def _dsa_index(iqh, ik, iw, *, q_off, topk, tq, tk):
    bsz, _, tlen, _ = iqh.shape
    klen = ik.shape[1]
    nk = klen // tk
    body = functools.partial(_dsa_index_body, tq=tq, tk=tk, q_off=q_off, nk=nk, topk=topk,
                             idx_bits=int(klen).bit_length())
    return pl.pallas_call(
        body, grid=(bsz, tlen // tq),
        in_specs=[pl.BlockSpec((None, N_IDX_HEADS, tq, IDX_DIM), lambda b, i: (b, 0, i, 0)),
                  pl.BlockSpec((None, klen, IDX_DIM), lambda b, i: (b, 0, 0)),
                  pl.BlockSpec((None, tq, N_IDX_HEADS), lambda b, i: (b, i, 0))],
        out_specs=pl.BlockSpec((None, nk, tq, tk), lambda b, i: (b, 0, i, 0)),
        out_shape=jax.ShapeDtypeStruct((bsz, nk, tlen, tk), F32),
        scratch_shapes=[pltpu.VMEM((nk, tq, tk), F32)],
        compiler_params=_cparams("parallel", "parallel"), name="dsa_index")(iqh, ik, iw)


def _gather_body(pt_ref, pool_ref, tail_ref, o_ref, sem, *, n_pages, n_tail):
    b = pl.program_id(0)

    def page_copy(p):
        return pltpu.make_async_copy(pool_ref.at[pt_ref[b, p]], o_ref.at[b, pl.ds(p * PAGE, PAGE)], sem.at[0])

    tail_copy = pltpu.make_async_copy(tail_ref.at[b], o_ref.at[b, pl.ds(n_pages * PAGE, n_tail)], sem.at[1])
    tail_copy.start()

    def start(p, c):
        page_copy(p).start()
        return c

    lax.fori_loop(0, n_pages, start, 0)

    def wait(p, c):
        page_copy(p).wait()
        return c

    lax.fori_loop(0, n_pages, wait, 0)
    tail_copy.wait()


def _gather(pool, page_table, tail):
    bsz, n_pages = page_table.shape
    n_tail, feat = tail.shape[1], tail.shape[2]
    body = functools.partial(_gather_body, n_pages=n_pages, n_tail=n_tail)
    return pl.pallas_call(
        body,
        grid_spec=pltpu.PrefetchScalarGridSpec(
            num_scalar_prefetch=1, grid=(bsz,),
            in_specs=[pl.BlockSpec(memory_space=pl.ANY), pl.BlockSpec(memory_space=pl.ANY)],
            out_specs=pl.BlockSpec(memory_space=pl.ANY),
            scratch_shapes=[pltpu.SemaphoreType.DMA((2,))]),
        out_shape=jax.ShapeDtypeStruct((bsz, n_pages * PAGE + n_tail, feat), F32),
        compiler_params=_cparams("arbitrary"), name="page_gather")(page_table, pool, tail)


def _rmajor_perm():
    r, g, d = np.meshgrid(np.arange(N_REP), np.arange(N_KV), np.arange(HEAD_DIM), indexing="ij")
    return ((N_REP * g + r) * HEAD_DIM + d).reshape(-1)


def _pad_cols(w, width):
    return jnp.pad(w, ((0, 0), (0, width - w.shape[1])))


def _out_weights(w_o, rmajor):
    wa = w_o[:QW]
    if rmajor:
        wa = wa[_rmajor_perm()]
    return wa.astype(BF16), w_o[QW:].astype(BF16)


def _overlap(ncp, nsp, n_cmp, n_sel):
    cs = np.arange(ncp)[:, None] * CMP_STRIDE
    ss = np.arange(nsp)[None, :] * SEL_BLOCK
    ov = np.clip(np.minimum(cs + CMP_LEN, ss + SEL_BLOCK) - np.maximum(cs, ss), 0, None).astype(np.float32) / CMP_LEN
    ov = ov * (np.arange(ncp)[:, None] < n_cmp) * (np.arange(nsp)[None, :] < n_sel)
    return jnp.asarray(ov, BF16)


def _compress_weights(pos, w1, w2):
    eye = jnp.eye(N_KV, dtype=F32)
    w1r = w1.reshape(CMP_LEN, HEAD_DIM, HEAD_DIM)
    half = CMP_LEN // 2

    def scatter(w):
        return jnp.einsum("pdf,gh->pgdhf", w, eye).reshape(half * KVW, KVW).astype(BF16)

    def tile_pos(p):
        return jnp.broadcast_to(p[:, None, :], (half, N_KV, HEAD_DIM)).reshape(1, half * KVW)

    return (tile_pos(pos[:half]), tile_pos(pos[half:]), scatter(w1r[:half]), scatter(w1r[half:]),
            jnp.kron(eye, w2).astype(BF16))


def _pad_q(x, tq):
    return jnp.pad(x, ((0, 0), (0, tq - x.shape[1]), (0, 0)))


def _tail(new, n_tail):
    return jnp.pad(new, ((0, 0), (0, n_tail - new.shape[1]), (0, 0)))


ST = 8
TAIL = 256


def _sb_layer(xp, xs, w_in, pool_k, pool_v, page_table, dims):
    bsz, seq, dbs, dseq = dims
    past = page_table.shape[1] * PAGE
    ws = [w_in[:, :QW], w_in[:, QW:2 * QW], w_in[:, 2 * QW:3 * QW], w_in[:, 3 * QW:]]
    ws = [w.astype(BF16) for w in ws]
    q_p, k_p, v_p, mq_p = _proj(xp, ws, 256)
    q_s, k_s, v_s, mq_s = _proj(xs, ws, xs.shape[0])
    mix_p = _sb_attn(q_p.reshape(bsz, seq, QW), k_p.reshape(bsz, seq, QW), v_p.reshape(bsz, seq, QW), 0, 256, 256)
    k_all = _gather(pool_k.reshape(pool_k.shape[0], PAGE, QW), page_table, _tail(k_s.reshape(dbs, dseq, QW), TAIL))
    v_all = _gather(pool_v.reshape(pool_v.shape[0], PAGE, QW), page_table, _tail(v_s.reshape(dbs, dseq, QW), TAIL))
    mix_s = _sb_attn(_pad_q(q_s.reshape(dbs, dseq, QW), ST), k_all, v_all, past, ST, 256)[:, :dseq]
    state = dict(sb_k_p=k_p.reshape(bsz, seq, N_HEADS, HEAD_DIM), sb_v_p=v_p.reshape(bsz, seq, N_HEADS, HEAD_DIM),
                 sb_k_s=k_s.reshape(dbs, dseq, N_HEADS, HEAD_DIM), sb_v_s=v_s.reshape(dbs, dseq, N_HEADS, HEAD_DIM))
    return mix_p.reshape(-1, QW), mix_s.reshape(-1, QW), mq_p, mq_s, state, False


def _nsa_layer(xp, xs, w_in, pools, win_k, win_v, page_table, pos, w1, w2, dims):
    bsz, seq, dbs, dseq = dims
    past = page_table.shape[1] * PAGE
    perm = _rmajor_perm()
    o = QW
    ws = [w_in[:, :QW][:, perm]]
    for _ in range(6):
        ws.append(w_in[:, o:o + KVW])
        o += KVW
    j, r, g, d = np.meshgrid(np.arange(3), np.arange(N_REP), np.arange(N_KV), np.arange(HEAD_DIM), indexing="ij")
    ws.append(w_in[:, o:o + 3 * N_HEADS][:, (g * (3 * N_REP) + r * 3 + j).reshape(-1)])
    o += 3 * N_HEADS
    ws.append(w_in[:, o:o + MQW])
    ws = [w.astype(BF16) for w in ws]
    q_p, ck_p, cv_p, sk_p, sv_p, wk_p, wv_p, gl_p, mq_p = _proj(xp, ws, 256)
    q_s, ck_s, cv_s, sk_s, sv_s, wk_s, wv_s, gl_s, mq_s = _proj(xs, ws, xs.shape[0])
    cw = [_compress_weights(pos[c], w1[c], w2[c]) for c in range(2)]
    b3 = lambda a, n: a.reshape(n, -1, a.shape[-1])

    n_cmp, n_sel = seq // CMP_STRIDE - 1, seq // SEL_BLOCK
    q3 = b3(q_p, bsz)
    kcmp = _compress(ck_p.reshape(bsz, seq // CMP_STRIDE, CMP_STRIDE * KVW), *cw[0], n_cmp, 128)
    vcmp = _compress(cv_p.reshape(bsz, seq // CMP_STRIDE, CMP_STRIDE * KVW), *cw[1], n_cmp, 128)
    nsp = -(-n_sel // LANES) * LANES
    o_c, sel = _nsa_cmp(q3, kcmp, vcmp, _overlap(kcmp.shape[1], nsp, n_cmp, n_sel), q_off=0, n_cmp=n_cmp,
                        n_sel=n_sel, tq=128)
    o_s = _alibi_attn(q3, b3(sk_p, bsz), b3(sv_p, bsz), sel, mode="sel", q_off=0, kbase=0, n_valid=seq, tq=128, tk=256)
    o_w = _alibi_attn(q3, b3(wk_p, bsz), b3(wv_p, bsz), None, mode="win", q_off=0, kbase=0, n_valid=seq, tq=128, tk=256)
    mix_p = _gate_combine(gl_p, o_c.reshape(-1, QW), o_s.reshape(-1, QW), o_w.reshape(-1, QW), 512)

    total = past + dseq
    lg = past + TAIL
    n_cmp_s = -(-total // SEL_BLOCK) * SEL_BLOCK // CMP_STRIDE - 1
    n_sel_s = -(-total // SEL_BLOCK)
    full = [_gather(pool.reshape(pool.shape[0], PAGE, KVW), page_table, _tail(b3(new, dbs), TAIL))
            for pool, new in zip(pools, (ck_s, cv_s, sk_s, sv_s))]
    q3s = _pad_q(b3(q_s, dbs), ST)
    crow = lg // CMP_STRIDE
    rows_s = next(rw for rw in range(256, 7, -8) if crow % rw == 0)
    kcmp_s = _compress(full[0].reshape(dbs, crow, CMP_STRIDE * KVW), *cw[0], n_cmp_s, rows_s)
    vcmp_s = _compress(full[1].reshape(dbs, crow, CMP_STRIDE * KVW), *cw[1], n_cmp_s, rows_s)
    ncp_s = -(-crow // LANES) * LANES
    kcmp_s = jnp.pad(kcmp_s, ((0, 0), (0, ncp_s - crow), (0, 0)))
    vcmp_s = jnp.pad(vcmp_s, ((0, 0), (0, ncp_s - crow), (0, 0)))
    nsp_s = -(-(lg // SEL_BLOCK) // LANES) * LANES
    o_c_s, sel_s = _nsa_cmp(q3s, kcmp_s, vcmp_s, _overlap(ncp_s, nsp_s, n_cmp_s, n_sel_s), q_off=past,
                            n_cmp=n_cmp_s, n_sel=n_sel_s, tq=ST)
    o_s_s = _alibi_attn(q3s, full[2], full[3], sel_s, mode="sel", q_off=past, kbase=0, n_valid=total, tq=ST, tk=256)
    wb = win_k.shape[1]
    kw = jnp.concatenate([win_k.reshape(dbs, wb, KVW), b3(wk_s, dbs)], axis=1)
    vw = jnp.concatenate([win_v.reshape(dbs, wb, KVW), b3(wv_s, dbs)], axis=1)
    lw = -(-kw.shape[1] // 256) * 256
    o_w_s = _alibi_attn(q3s, _tail(kw, lw), _tail(vw, lw), None, mode="win", q_off=past, kbase=past - wb,
                        n_valid=kw.shape[1], tq=ST, tk=256)
    gl_s8 = _pad_q(b3(gl_s, dbs), ST).reshape(-1, 3 * QW)
    mix_s = _gate_combine(gl_s8, o_c_s.reshape(-1, QW), o_s_s.reshape(-1, QW), o_w_s.reshape(-1, QW), dbs * ST)
    mix_s = mix_s.reshape(dbs, ST, QW)[:, :dseq].reshape(-1, QW)

    h4 = lambda a, n: a.reshape(n, -1, N_KV, HEAD_DIM)
    wkeep = min(WINDOW, seq)
    state = dict(ck_p=h4(ck_p, bsz), cv_p=h4(cv_p, bsz), sk_p=h4(sk_p, bsz), sv_p=h4(sv_p, bsz),
                 ck_s=h4(ck_s, dbs), cv_s=h4(cv_s, dbs), sk_s=h4(sk_s, dbs), sv_s=h4(sv_s, dbs),
                 wk_p=h4(wk_p, bsz)[:, seq - wkeep:], wv_p=h4(wv_p, bsz)[:, seq - wkeep:],
                 wk_s=h4(kw[:, kw.shape[1] - wb:], dbs), wv_s=h4(vw[:, vw.shape[1] - wb:], dbs))
    return mix_p, mix_s, mq_p, mq_s, state, True


def _dsa_layer(xp, xs, w_in, pool_k, pool_v, pool_ik, page_table, dims):
    bsz, seq, dbs, dseq = dims
    past = page_table.shape[1] * PAGE
    iqw = N_IDX_HEADS * IDX_DIM
    o = QW
    ws = [w_in[:, :QW][:, _rmajor_perm()], w_in[:, o:o + KVW], w_in[:, o + KVW:o + 2 * KVW]]
    o += 2 * KVW
    ws.append(w_in[:, o:o + iqw])
    o += iqw
    ws.append(_pad_cols(w_in[:, o:o + IDX_DIM], LANES))
    o += IDX_DIM
    ws.append(_pad_cols(w_in[:, o:o + N_IDX_HEADS], LANES))
    o += N_IDX_HEADS
    ws.append(w_in[:, o:o + MQW])
    ws = [w.astype(BF16) for w in ws]
    q_p, k_p, v_p, iq_p, ik_p, iw_p, mq_p = _proj(xp, ws, 256)
    q_s, k_s, v_s, iq_s, ik_s, iw_s, mq_s = _proj(xs, ws, xs.shape[0])
    ik_p, ik_s = ik_p[:, :IDX_DIM], ik_s[:, :IDX_DIM]
    iw_p, iw_s = iw_p[:, :N_IDX_HEADS], iw_s[:, :N_IDX_HEADS]
    b3 = lambda a, n: a.reshape(n, -1, a.shape[-1])
    heads_first = lambda a, n: a.reshape(n, -1, N_IDX_HEADS, IDX_DIM).transpose(0, 2, 1, 3)

    mask = _dsa_index(heads_first(iq_p, bsz), b3(ik_p, bsz), b3(iw_p, bsz), q_off=0,
                      topk=min(DSA_TOPK_MAX, seq // 4), tq=128, tk=256)
    mix_p = _alibi_attn(b3(q_p, bsz), b3(k_p, bsz), b3(v_p, bsz), mask, mode="mask", q_off=0, kbase=0,
                        n_valid=seq, tq=128, tk=256)

    total = past + dseq
    k_all = _gather(pool_k.reshape(pool_k.shape[0], PAGE, KVW), page_table, _tail(b3(k_s, dbs), TAIL))
    v_all = _gather(pool_v.reshape(pool_v.shape[0], PAGE, KVW), page_table, _tail(b3(v_s, dbs), TAIL))
    ik_all = _gather(pool_ik, page_table, _tail(b3(ik_s, dbs), TAIL))
    q3s = _pad_q(b3(q_s, dbs), ST)
    iqs = _pad_q(b3(iq_s, dbs), ST)
    iws = _pad_q(b3(iw_s, dbs), ST)
    mask_s = _dsa_index(heads_first(iqs, dbs), ik_all, iws, q_off=past, topk=min(DSA_TOPK_MAX, total // 4),
                        tq=ST, tk=256)
    mix_s = _alibi_attn(q3s, k_all, v_all, mask_s, mode="mask", q_off=past, kbase=0, n_valid=total, tq=ST, tk=256)
    mix_s = mix_s[:, :dseq].reshape(-1, QW)

    h4 = lambda a, n: a.reshape(n, -1, N_KV, HEAD_DIM)
    state = dict(dk_p=h4(k_p, bsz), dv_p=h4(v_p, bsz), dik_p=b3(ik_p, bsz),
                 dk_s=h4(k_s, dbs), dv_s=h4(v_s, dbs), dik_s=b3(ik_s, dbs))
    return mix_p.reshape(-1, QW), mix_s, mq_p, mq_s, state, True


_STATE_ORDER = ("sb_k_p", "sb_v_p", "sb_k_s", "sb_v_s", "ck_p", "cv_p", "sk_p", "sv_p", "ck_s", "cv_s", "sk_s", "sv_s",
                "wk_p", "wv_p", "wk_s", "wv_s", "dk_p", "dv_p", "dik_p", "dk_s", "dv_s", "dik_s", "mk_p", "mv_p")


def kernel(x_prompt, x_sample, cache_sb_k, cache_sb_v, cache_nsa_ck, cache_nsa_cv, cache_nsa_sk, cache_nsa_sv,
           state_nsa_wk, state_nsa_wv, cache_dsa_k, cache_dsa_v, cache_dsa_ik, cache_mem_k, cache_mem_v, page_table,
           mem_prompt, sb_w_in, nsa_w_in, dsa_w_in, nsa_cmp_pos, nsa_cmp_w1, nsa_cmp_w2, w_mem_kv, w_out, ln_g, ln_b,
           router_w, router_b, moe_w_gate, moe_w_up, moe_w_down):
    bsz, seq, _ = x_prompt.shape
    dbs, dseq, _ = x_sample.shape
    dims = (bsz, seq, dbs, dseq)
    depth = w_out.shape[0]
    n_mem = mem_prompt.shape[1]
    xp = x_prompt.reshape(bsz * seq, D_MODEL)
    xs = x_sample.reshape(dbs * dseq, D_MODEL)
    memp = mem_prompt.reshape(bsz * n_mem, D_MODEL)
    rwt = router_w.T.astype(BF16)
    rb = router_b.reshape(N_EXPERTS, 1)
    st = {nm: [] for nm in _STATE_ORDER}
    for i in range(depth):
        kind, j = i % 3, i // 3
        if kind == 0:
            res = _sb_layer(xp, xs, sb_w_in[j], cache_sb_k[j], cache_sb_v[j], page_table, dims)
        elif kind == 1:
            res = _nsa_layer(xp, xs, nsa_w_in[j],
                             (cache_nsa_ck[j], cache_nsa_cv[j], cache_nsa_sk[j], cache_nsa_sv[j]),
                             state_nsa_wk[j], state_nsa_wv[j], page_table,
                             nsa_cmp_pos[j], nsa_cmp_w1[j], nsa_cmp_w2[j], dims)
        else:
            res = _dsa_layer(xp, xs, dsa_w_in[j], cache_dsa_k[j], cache_dsa_v[j], cache_dsa_ik[j], page_table, dims)
        mix_p, mix_s, mq_p, mq_s, state, rmajor = res
        for nm, a in state.items():
            st[nm].append(a)
        wkv = w_mem_kv[i]
        mk_p, mv_p = _proj(memp, [wkv[:, :MQW].astype(BF16), wkv[:, MQW:].astype(BF16)], 256)
        st["mk_p"].append(mk_p.reshape(bsz, n_mem, N_MEM_HEADS, HEAD_DIM))
        st["mv_p"].append(mv_p.reshape(bsz, n_mem, N_MEM_HEADS, HEAD_DIM))
        mem_p = _mem_attn(mq_p.reshape(bsz, seq, MQW), mk_p.reshape(bsz, n_mem, MQW), mv_p.reshape(bsz, n_mem, MQW), 512)
        mem_s = _mem_attn(_pad_q(mq_s.reshape(dbs, dseq, MQW), ST), cache_mem_k[i].reshape(dbs, n_mem, MQW),
                          cache_mem_v[i].reshape(dbs, n_mem, MQW), ST)[:, :dseq]
        wa, wb = _out_weights(w_out[i], rmajor)
        g0, b0 = ln_g[i, 0].reshape(1, D_MODEL), ln_b[i, 0].reshape(1, D_MODEL)
        g1, b1 = ln_g[i, 1].reshape(1, D_MODEL), ln_b[i, 1].reshape(1, D_MODEL)
        xp = _out_ln(xp, mix_p, mem_p.reshape(-1, MQW), wa, wb, g0, b0, 512)
        xs = _out_ln(xs, mix_s, mem_s.reshape(-1, MQW), wa, wb, g0, b0, xs.shape[0])
        xp = _moe_ln(xp, rwt, rb, moe_w_gate[i], moe_w_up[i], moe_w_down[i], g1, b1, 1024)
        xs = _moe_ln(xs, rwt, rb, moe_w_gate[i], moe_w_up[i], moe_w_down[i], g1, b1, xs.shape[0])
    out = [xp.reshape(bsz, seq, D_MODEL), xs.reshape(dbs, dseq, D_MODEL)]
    out += [jnp.stack(st[nm]) for nm in _STATE_ORDER]
    return tuple(out)
```
